```python
import math
import jax, jax.numpy as jnp
from jax import lax
import numpy as np

D_MODEL = 2048
BATCH = 2
SEQ = 8192
DEPTH = 4

GRID_W = 64
CTX_LEN = 256
CONV_K = 3
SC_WIDTH = D_MODEL
MLA_HEADS = 16
Q_LORA = 512
KV_LORA = 512
QK_NOPE = 128
QK_ROPE = 64
V_DIM = 128
ROPE_BASE = 10000.0
Q_BLOCK = 128
MLA_SCALE = (QK_NOPE + QK_ROPE) ** -0.5
SSM_INNER = 2 * D_MODEL
SSM_HEADDIM = 64
SSM_HEADS = SSM_INNER // SSM_HEADDIM
SSM_GROUPS = 8
SSM_STATE = 128
SSM_CHUNK = 128
SSM_CONV_CH = SSM_INNER + 2 * SSM_GROUPS * SSM_STATE
D_FF = 5632
N_BRANCH = 3
ALPHA = (2 * DEPTH) ** 0.25
BETA = (8 * DEPTH) ** -0.25
EPS = 1e-6
IN_SPLITS = (SC_WIDTH, SC_WIDTH, SC_WIDTH,
             Q_LORA, KV_LORA, QK_ROPE,
             SSM_INNER, SSM_INNER, SSM_GROUPS * SSM_STATE, SSM_GROUPS * SSM_STATE, 2 * SSM_HEADS,
             N_BRANCH * D_MODEL)
D_IN = sum(IN_SPLITS)
SPLIT_POINTS = tuple(int(v) for v in np.cumsum(IN_SPLITS)[:-1])

kernel_name = 'hybrid_flow_backbone'


def layer_norm(x):
    xf = x.astype(jnp.float32)
    mu = xf.mean(-1, keepdims=True)
    var = jnp.square(xf - mu).mean(-1, keepdims=True)
    return ((xf - mu) * lax.rsqrt(var + EPS)).astype(x.dtype)


def post_norm(x, g, b):
    return layer_norm(x) * g + b


def rms_norm(x, g):
    xf = x.astype(jnp.float32)
    y = xf * lax.rsqrt(jnp.square(xf).mean(-1, keepdims=True) + EPS)
    return y.astype(x.dtype) * g


def modulate(x, shift, scale):
    return layer_norm(x) * (1 + scale) + shift


def dwconv(x, w, b=None):
    k = w.shape[0]
    n = x.shape[1]
    pad = k // 2
    xp = jnp.pad(x, ((0, 0), (pad, pad), (0, 0)))
    y = xp[:, 0:n] * w[0]
    for i in range(1, k):
        y = y + xp[:, i:i + n] * w[i]
    return y if b is None else y + b


def axial_rope_tables(n_tokens):
    rows_n = n_tokens // GRID_W
    row = jnp.repeat(jnp.arange(rows_n), GRID_W).astype(jnp.float32)
    col = jnp.tile(jnp.arange(GRID_W), rows_n).astype(jnp.float32)
    nf = QK_ROPE // 4
    inv = ROPE_BASE ** (-jnp.arange(nf, dtype=jnp.float32) / nf)
    ang = jnp.stack([row[:, None] * inv, col[:, None] * inv], axis=1)
    return jnp.cos(ang), jnp.sin(ang)


def apply_rope(x, cos, sin):
    xr = x.reshape(x.shape[:-1] + (2, 2, QK_ROPE // 4))
    x1, x2 = xr[..., 0, :], xr[..., 1, :]
    cos = cos.astype(x.dtype)
    sin = sin.astype(x.dtype)
    out = jnp.stack([x1 * cos - x2 * sin, x2 * cos + x1 * sin], axis=-2)
    return out.reshape(x.shape)


def mla_q(q_lat, p):
    cq = rms_norm(q_lat, p['g_qa'])
    q = (cq @ p['w_uq']).reshape(q_lat.shape[:-1] + (MLA_HEADS, QK_NOPE + QK_ROPE))
    return q[..., :QK_NOPE], q[..., QK_NOPE:]


def mla_kv(kv_lat, p):
    ckv = rms_norm(kv_lat, p['g_kva'])
    kv = (ckv @ p['w_ukv']).reshape(kv_lat.shape[:-1] + (MLA_HEADS, QK_NOPE + V_DIM))
    return kv[..., :QK_NOPE], kv[..., QK_NOPE:]


def mla_attend(q_nope, q_pe, k_nope, k_pe, v):
    s = (jnp.einsum('bqhd,bkhd->bhqk', q_nope, k_nope)
         + jnp.einsum('bqhr,bkr->bhqk', q_pe, k_pe))
    prob = jax.nn.softmax(s.astype(jnp.float32) * MLA_SCALE, axis=-1).astype(v.dtype)
    return jnp.einsum('bhqk,bkhd->bqhd', prob, v)


def mla_blocked(q_nope, q_pe, k_nope, k_pe, v):
    b, s = q_nope.shape[:2]
    nb = s // Q_BLOCK

    def blocks(t):
        return jnp.swapaxes(t.reshape((b, nb, Q_BLOCK) + t.shape[2:]), 0, 1)

    out = lax.map(lambda qs: mla_attend(qs[0], qs[1], k_nope, k_pe, v),
                  (blocks(q_nope), blocks(q_pe)))
    return jnp.swapaxes(out, 0, 1).reshape(b, s, MLA_HEADS * V_DIM)


def segsum(a):
    t = a.shape[-1]
    cs = jnp.cumsum(a, axis=-1)
    diff = cs[..., :, None] - cs[..., None, :]
    return jnp.where(jnp.tril(jnp.ones((t, t), dtype=bool)), diff, -jnp.inf)


def ssd_scan(x, dt, a, bm, cm, h0):
    b, l, h, pd = x.shape
    g, n = bm.shape[2:]
    r = h // g
    q = SSM_CHUNK
    c = l // q
    f32 = jnp.float32
    xq = x.astype(f32).reshape(b, c, q, g, r, pd)
    dtq = dt.astype(f32).reshape(b, c, q, g, r)
    bq = bm.astype(f32).reshape(b, c, q, g, n)
    cq = cm.astype(f32).reshape(b, c, q, g, n)
    xdt = xq * dtq[..., None]
    adt = jnp.transpose(dtq * a.reshape(g, r), (0, 3, 4, 1, 2))
    a_cs = jnp.cumsum(adt, axis=-1)
    lmat = jnp.exp(segsum(adt))
    cb = jnp.einsum('bclgn,bcsgn->bgcls', cq, bq)
    y_diag = jnp.einsum('bgcls,bgrcls,bcsgrp->bclgrp', cb, lmat, xdt)
    decay_s = jnp.exp(a_cs[..., -1:] - a_cs)
    states = jnp.einsum('bcsgn,bgrcs,bcsgrp->bcgrpn', bq, decay_s, xdt)
    states = jnp.concatenate([h0.astype(f32)[:, None], states], axis=1)
    chunk_a = jnp.pad(a_cs[..., -1], ((0, 0), (0, 0), (0, 0), (1, 0)))
    decay_c = jnp.exp(segsum(chunk_a))
    states = jnp.einsum('bgrzc,bcgrpn->bzgrpn', decay_c, states)
    prev, final = states[:, :-1], states[:, -1]
    y_off = jnp.einsum('bclgn,bcgrpn,bgrcl->bclgrp', cq, prev, jnp.exp(a_cs))
    y = (y_diag + y_off).reshape(b, l, h, pd)
    return y.astype(x.dtype), final


def ssm_inputs(xs, bs, cs, dt_raw, p):
    xbc = jax.nn.silu(dwconv(jnp.concatenate([xs, bs, cs], axis=-1), p['w_ssm_conv'], p['b_ssm_conv']))
    xh, bm, cm = jnp.split(xbc, [SSM_INNER, SSM_INNER + SSM_GROUPS * SSM_STATE], axis=-1)
    lead = xs.shape[:2]
    xh = xh.reshape(lead + (SSM_HEADS, SSM_HEADDIM))
    bm = bm.reshape(lead + (SSM_GROUPS, SSM_STATE))
    cm = cm.reshape(lead + (SSM_GROUPS, SSM_STATE))
    dt_f = jax.nn.softplus(dt_raw[..., :SSM_HEADS] + p['dt_bias_f'])
    dt_b = jax.nn.softplus(dt_raw[..., SSM_HEADS:] + p['dt_bias_b'])
    return xh, bm, cm, dt_f, dt_b


def ssm_bidir(inp, h0_f, h0_b, p):
    xh, bm, cm, dt_f, dt_b = inp
    a_f = -jnp.exp(p['a_log_f'].astype(jnp.float32))
    a_b = -jnp.exp(p['a_log_b'].astype(jnp.float32))
    y_f, h_f = ssd_scan(xh, dt_f, a_f, bm, cm, h0_f)

    def fl(t):
        return jnp.flip(t, axis=1)

    y_b, h_b = ssd_scan(fl(xh), fl(dt_b), a_b, fl(bm), fl(cm), h0_b)
    y = y_f + fl(y_b) + p['d_skip'][:, None] * xh
    return y, h_f, h_b


def ssm_out(y, z, p):
    y = y.reshape(z.shape)
    return rms_norm(y * jax.nn.silu(z), p['g_ssm_norm']) @ p['w_ssm_out']


def merge_branches(gates, ya, yb, yc, w_out):
    g = jax.nn.sigmoid(gates.reshape(gates.shape[:-1] + (N_BRANCH, D_MODEL)))
    return (g[..., 0, :] * ya + g[..., 1, :] * yb + g[..., 2, :] * yc) @ w_out


def token_mix(u_x, u_c, p, cos, sin, need_ctx):
    (sb_x, sc_x, sh_x, ql_x, kvl_x, kr_x, z_x, xs_x, bs_x, cs_x, dt_x, gt_x) = jnp.split(
        u_x @ p['w_in'], SPLIT_POINTS, axis=-1)
    (sb_c, sc_c, sh_c, ql_c, kvl_c, kr_c, z_c, xs_c, bs_c, cs_c, dt_c, gt_c) = jnp.split(
        u_c @ p['w_in'], SPLIT_POINTS, axis=-1)
    b = u_x.shape[0]
    ya_x = (sb_x * dwconv(sc_x * sh_x, p['w_sc_conv'])) @ p['w_sc_out']
    kn_c, v_c = mla_kv(kvl_c, p)
    kn_x, v_x = mla_kv(kvl_x, p)
    qn_x, qp_x = mla_q(ql_x, p)
    qp_x = apply_rope(qp_x, cos[:, None], sin[:, None])
    kr_xr = apply_rope(kr_x, cos, sin)
    o_x = mla_blocked(qn_x, qp_x,
                      jnp.concatenate([kn_c, kn_x], axis=1),
                      jnp.concatenate([kr_c, kr_xr], axis=1),
                      jnp.concatenate([v_c, v_x], axis=1))
    yb_x = o_x @ p['w_mla_out']
    h0 = jnp.zeros((b, SSM_GROUPS, SSM_HEADS // SSM_GROUPS, SSM_HEADDIM, SSM_STATE), jnp.float32)
    y_ssm_c, h_f, h_b = ssm_bidir(ssm_inputs(xs_c, bs_c, cs_c, dt_c, p), h0, h0, p)
    y_ssm_x, _, _ = ssm_bidir(ssm_inputs(xs_x, bs_x, cs_x, dt_x, p), h_f, h_b, p)
    yc_x = ssm_out(y_ssm_x, z_x, p)
    out_x = merge_branches(gt_x, ya_x, yb_x, yc_x, p['w_out'])
    if not need_ctx:
        return out_x, None
    ya_c = (sb_c * dwconv(sc_c * sh_c, p['w_sc_conv'])) @ p['w_sc_out']
    qn_c, qp_c = mla_q(ql_c, p)
    o_c = mla_attend(qn_c, qp_c, kn_c, kr_c, v_c)
    yb_c = o_c.reshape(o_c.shape[:2] + (MLA_HEADS * V_DIM,)) @ p['w_mla_out']
    yc_c = ssm_out(y_ssm_c, z_c, p)
    out_c = merge_branches(gt_c, ya_c, yb_c, yc_c, p['w_out'])
    return out_x, out_c


def conv_ffn(u, p):
    gate, val = jnp.split(u @ p['w_up'], 2, axis=-1)
    return (jax.nn.silu(dwconv(gate, p['w_ff_conv'])) * val) @ p['w_down']


def setup_inputs(seed: int = 0) -> dict:
    key = jax.random.key(seed)
    keys = jax.random.split(key, 32)
    f32 = jnp.float32
    L = DEPTH

    def nrm(i, shape, scale):
        return jax.random.normal(keys[i], shape, f32) * scale

    def gain(i, shape):
        return 1.0 + nrm(i, shape, 0.01)

    def dt_bias(i):
        dt0 = jnp.exp(jax.random.uniform(keys[i], (L, SSM_HEADS), f32,
                                         minval=math.log(1e-3), maxval=math.log(1e-1)))
        return dt0 + jnp.log(-jnp.expm1(-dt0))

    def a_log(i):
        return jnp.log(jax.random.uniform(keys[i], (L, SSM_HEADS), f32, minval=1.0, maxval=16.0))

    return {
        'x': nrm(0, (BATCH, SEQ, D_MODEL), 1.0),
        'c': nrm(1, (BATCH, D_MODEL), 1.0),
        'ctx': nrm(2, (BATCH, CTX_LEN, D_MODEL), 1.0),
        'c_ctx': nrm(3, (D_MODEL,), 1.0),
        'w_ada': nrm(4, (L, D_MODEL, 6 * D_MODEL), 0.5 * D_MODEL ** -0.5),
        'b_ada': nrm(5, (L, 6 * D_MODEL), 0.01),
        'w_in': nrm(6, (L, D_MODEL, D_IN), D_MODEL ** -0.5),
        'w_sc_conv': nrm(7, (L, CONV_K, SC_WIDTH), CONV_K ** -0.5),
        'w_sc_out': nrm(8, (L, SC_WIDTH, D_MODEL), SC_WIDTH ** -0.5),
        'g_qa': gain(9, (L, Q_LORA)),
        'w_uq': nrm(10, (L, Q_LORA, MLA_HEADS * (QK_NOPE + QK_ROPE)), Q_LORA ** -0.5),
        'g_kva': gain(11, (L, KV_LORA)),
        'w_ukv': nrm(12, (L, KV_LORA, MLA_HEADS * (QK_NOPE + V_DIM)), KV_LORA ** -0.5),
        'w_mla_out': nrm(13, (L, MLA_HEADS * V_DIM, D_MODEL), (MLA_HEADS * V_DIM) ** -0.5),
        'w_ssm_conv': nrm(14, (L, CONV_K, SSM_CONV_CH), CONV_K ** -0.5),
        'b_ssm_conv': nrm(15, (L, SSM_CONV_CH), 0.01),
        'dt_bias_f': dt_bias(16),
        'dt_bias_b': dt_bias(17),
        'a_log_f': a_log(18),
        'a_log_b': a_log(19),
        'd_skip': gain(20, (L, SSM_HEADS)),
        'g_ssm_norm': gain(21, (L, SSM_INNER)),
        'w_ssm_out': nrm(22, (L, SSM_INNER, D_MODEL), SSM_INNER ** -0.5),
        'w_out': nrm(23, (L, D_MODEL, D_MODEL), BETA * D_MODEL ** -0.5),
        'ln1_g': gain(24, (L, D_MODEL)),
        'ln1_b': nrm(25, (L, D_MODEL), 0.01),
        'w_up': nrm(26, (L, D_MODEL, 2 * D_FF), D_MODEL ** -0.5),
        'w_ff_conv': nrm(27, (L, CONV_K, D_FF), CONV_K ** -0.5),
        'w_down': nrm(28, (L, D_FF, D_MODEL), BETA * D_FF ** -0.5),
        'ln2_g': gain(29, (L, D_MODEL)),
        'ln2_b': nrm(30, (L, D_MODEL), 0.01),
    }


def reference(x, c, ctx, c_ctx, w_ada, b_ada, w_in, w_sc_conv, w_sc_out, g_qa, w_uq, g_kva, w_ukv,
              w_mla_out, w_ssm_conv, b_ssm_conv, dt_bias_f, dt_bias_b, a_log_f, a_log_b, d_skip,
              g_ssm_norm, w_ssm_out, w_out, ln1_g, ln1_b, w_up, w_ff_conv, w_down, ln2_g, ln2_b):
    cos, sin = axial_rope_tables(x.shape[1])
    sc = jax.nn.silu(c)
    scc = jax.nn.silu(c_ctx)
    for l in range(DEPTH):
        need_ctx = l < DEPTH - 1
        p = {
            'w_in': w_in[l], 'w_sc_conv': w_sc_conv[l], 'w_sc_out': w_sc_out[l],
            'g_qa': g_qa[l], 'w_uq': w_uq[l], 'g_kva': g_kva[l], 'w_ukv': w_ukv[l],
            'w_mla_out': w_mla_out[l], 'w_ssm_conv': w_ssm_conv[l], 'b_ssm_conv': b_ssm_conv[l],
            'dt_bias_f': dt_bias_f[l], 'dt_bias_b': dt_bias_b[l], 'a_log_f': a_log_f[l],
            'a_log_b': a_log_b[l], 'd_skip': d_skip[l], 'g_ssm_norm': g_ssm_norm[l],
            'w_ssm_out': w_ssm_out[l], 'w_out': w_out[l], 'w_up': w_up[l],
            'w_ff_conv': w_ff_conv[l], 'w_down': w_down[l],
        }
        mx = jnp.split((sc @ w_ada[l] + b_ada[l])[:, None, :], 6, axis=-1)
        mc = jnp.split(scc @ w_ada[l] + b_ada[l], 6, axis=-1)
        y_x, y_c = token_mix(modulate(x, mx[0], mx[1]), modulate(ctx, mc[0], mc[1]),
                             p, cos, sin, need_ctx)
        x = post_norm(ALPHA * x + mx[2] * y_x, ln1_g[l], ln1_b[l])
        x = post_norm(ALPHA * x + mx[5] * conv_ffn(modulate(x, mx[3], mx[4]), p), ln2_g[l], ln2_b[l])
        if need_ctx:
            ctx = post_norm(ALPHA * ctx + mc[2] * y_c, ln1_g[l], ln1_b[l])
            ctx = post_norm(ALPHA * ctx + mc[5] * conv_ffn(modulate(ctx, mc[3], mc[4]), p),
                            ln2_g[l], ln2_b[l])
    return x
```

```python
import functools
import math

import numpy as np
import jax
import jax.numpy as jnp
from jax import lax
from jax.experimental import pallas as pl
from jax.experimental.pallas import tpu as pltpu

GRID_W = 64
CONV_K = 3
MLA_HEADS = 16
Q_LORA = 512
KV_LORA = 512
QK_NOPE = 128
QK_ROPE = 64
V_DIM = 128
ROPE_BASE = 10000.0
SSM_HEADDIM = 64
SSM_GROUPS = 8
SSM_STATE = 128
SSM_CHUNK = 128
N_BRANCH = 3
EPS = 1e-6

LANES = 128
BF16_ROWS = 16
VMEM_CAP_BYTES = 56 * 2 ** 20
ROW_TILE = 256

F32 = jnp.float32
BF16 = jnp.bfloat16
QK_DIM = QK_NOPE + QK_ROPE
LOG2E = math.log2(math.e)


def _params(sem, block_bytes):
    limit = int(min(VMEM_CAP_BYTES, 2 * block_bytes + 24 * 2 ** 20))
    return pltpu.CompilerParams(dimension_semantics=sem, vmem_limit_bytes=limit)


def _nbytes(shape, dtype):
    return int(np.prod(shape)) * jnp.dtype(dtype).itemsize


def _tile(n, pref, unit=LANES):
    if n <= pref:
        return n
    best = None
    for t in range(unit, pref + 1, unit):
        if n % t == 0:
            best = t
    assert best is not None, (n, pref)
    return best


def _silu(v):
    return v * jax.nn.sigmoid(v)


def _layer_norm(v):
    mu = jnp.mean(v, axis=-1, keepdims=True)
    vc = v - mu
    var = jnp.mean(vc * vc, axis=-1, keepdims=True)
    return vc * lax.rsqrt(var + EPS)


def _rms(v):
    return v * lax.rsqrt(jnp.mean(v * v, axis=-1, keepdims=True) + EPS)


def _ada_kernel(c_ref, w_ref, b_ref, o_ref):
    s = _silu(c_ref[...])
    o_ref[...] = jnp.dot(s.astype(BF16), w_ref[...].astype(BF16), preferred_element_type=F32) + b_ref[...]


def _ada(cond8, w_ada, b_ada):
    depth, d, n = w_ada.shape
    tn = _tile(n, 1024)
    blocks = _nbytes((d, tn), F32) + _nbytes((8, d), F32) + _nbytes((8, tn), F32)
    return pl.pallas_call(
        _ada_kernel,
        grid=(depth, n // tn),
        in_specs=[pl.BlockSpec((8, d), lambda l, j: (0, 0)),
                  pl.BlockSpec((None, d, tn), lambda l, j: (l, 0, j)),
                  pl.BlockSpec((None, 1, tn), lambda l, j: (l, 0, j))],
        out_specs=pl.BlockSpec((None, 8, tn), lambda l, j: (l, 0, j)),
        out_shape=jax.ShapeDtypeStruct((depth, 8, n), F32),
        compiler_params=_params(("parallel", "parallel"), blocks),
        name="ada_mod",
    )(cond8, w_ada, b_ada.reshape(depth, 1, n))


class _Rows:
    def __init__(self, n_batch, ctx, seq):
        self.nb, self.ctx, self.seq = n_batch, ctx, seq
        self.lseq = ctx + seq
        self.rows = n_batch * self.lseq
        assert ctx % ROW_TILE == 0 and seq % ROW_TILE == 0
        self.tpb = self.lseq // ROW_TILE
        self.ct = ctx // ROW_TILE
        self.ntiles = self.rows // ROW_TILE

    def seg(self, i):
        return jnp.where(i % self.tpb < self.ct, self.nb, i // self.tpb)

    def mod_spec(self, k, d):
        return pl.BlockSpec((None, None, 1, d), lambda i: (k, self.seg(i), 0, 0))


def _lnmod_kernel(x_ref, sh_ref, sc_ref, u_ref):
    u_ref[...] = (_layer_norm(x_ref[...]) * (1.0 + sc_ref[...]) + sh_ref[...]).astype(u_ref.dtype)


def _lnmod(rg, x, mod, k_shift, k_scale):
    d = x.shape[1]
    row = pl.BlockSpec((ROW_TILE, d), lambda i: (i, 0))
    return pl.pallas_call(
        _lnmod_kernel,
        grid=(rg.ntiles,),
        in_specs=[row, rg.mod_spec(k_shift, d), rg.mod_spec(k_scale, d)],
        out_specs=row,
        out_shape=jax.ShapeDtypeStruct(x.shape, BF16),
        compiler_params=_params(("parallel",), 2 * _nbytes((ROW_TILE, d), F32)),
        name="ln_modulate",
    )(x, mod, mod)


def _postnorm_kernel(alpha, with_next, x_ref, y_ref, gate_ref, g_ref, b_ref, *rest):
    if with_next:
        sh_ref, sc_ref, xo_ref, u_ref = rest
    else:
        (xo_ref,) = rest
    v = alpha * x_ref[...] + gate_ref[...] * y_ref[...]
    xn = _layer_norm(v) * g_ref[...] + b_ref[...]
    xo_ref[...] = xn
    if with_next:
        u_ref[...] = (_layer_norm(xn) * (1.0 + sc_ref[...]) + sh_ref[...]).astype(u_ref.dtype)


def _postnorm(rg, alpha, x, y, mod, k_gate, ln_g, ln_b, next_mod=None, k_shift=None, k_scale=None):
    d = x.shape[1]
    row = pl.BlockSpec((ROW_TILE, d), lambda i: (i, 0))
    vec = pl.BlockSpec((1, d), lambda i: (0, 0))
    with_next = next_mod is not None
    in_specs = [row, row, rg.mod_spec(k_gate, d), vec, vec]
    args = [x, y, mod, ln_g.reshape(1, d), ln_b.reshape(1, d)]
    out_specs = [row]
    out_shape = [jax.ShapeDtypeStruct(x.shape, F32)]
    if with_next:
        in_specs += [rg.mod_spec(k_shift, d), rg.mod_spec(k_scale, d)]
        args += [next_mod, next_mod]
        out_specs.append(row)
        out_shape.append(jax.ShapeDtypeStruct(x.shape, BF16))
    out = pl.pallas_call(
        functools.partial(_postnorm_kernel, alpha, with_next),
        grid=(rg.ntiles,),
        in_specs=in_specs,
        out_specs=out_specs,
        out_shape=out_shape,
        compiler_params=_params(("parallel",), 4 * _nbytes((ROW_TILE, d), F32)),
        name="post_norm",
    )(*args)
    return (out[0], out[1]) if with_next else (out[0], None)


def _mm_kernel(a_ref, w_ref, o_ref):
    o_ref[...] = jnp.dot(a_ref[...], w_ref[...], preferred_element_type=F32).astype(o_ref.dtype)


def _matmul(a, w, out_dtype, name, tm_pref=768, tn_pref=1024):
    m, k = a.shape
    n = w.shape[1]
    tm = _tile(m, tm_pref, 8)
    tn = _tile(n, tn_pref)
    blocks = _nbytes((tm, k), BF16) + _nbytes((k, tn), BF16) + _nbytes((tm, tn), F32)
    return pl.pallas_call(
        _mm_kernel,
        grid=(m // tm, n // tn),
        in_specs=[pl.BlockSpec((tm, k), lambda i, j: (i, 0)),
                  pl.BlockSpec((k, tn), lambda i, j: (0, j))],
        out_specs=pl.BlockSpec((tm, tn), lambda i, j: (i, j)),
        out_shape=jax.ShapeDtypeStruct((m, n), out_dtype),
        compiler_params=_params(("parallel", "parallel"), blocks),
        name=name,
    )(a, w)


def _conv_kernel(n_in, has_bias, use_silu, has_post, tpb, ct, *refs):
    refs = list(refs)
    o_ref = refs.pop()
    i = pl.program_id(0)
    t = i % tpb
    starts_seq = (t == 0) | (t == ct)
    ends_seq = (t == ct - 1) | (t == tpb - 1)
    v = vp = vn = None
    for k in range(n_in):
        main = refs[3 * k][...].astype(F32)
        prev = refs[3 * k + 1][...].astype(F32)[BF16_ROWS - 1:BF16_ROWS, :]
        nxt = refs[3 * k + 2][...].astype(F32)[0:1, :]
        v = main if v is None else v * main
        vp = prev if vp is None else vp * prev
        vn = nxt if vn is None else vn * nxt
    rest = refs[3 * n_in:]
    w = rest.pop(0)[...]
    vp = jnp.where(starts_seq, 0.0, vp)
    vn = jnp.where(ends_seq, 0.0, vn)
    tm = v.shape[0]
    rows = lax.broadcasted_iota(jnp.int32, v.shape, 0)
    down = jnp.where(rows == 0, vp, pltpu.roll(v, 1, 0))
    up = jnp.where(rows == tm - 1, vn, pltpu.roll(v, tm - 1, 0))
    y = down * w[0:1, :] + v * w[1:2, :] + up * w[2:3, :]
    if has_bias:
        y = y + rest.pop(0)[...]
    if use_silu:
        y = _silu(y)
    if has_post:
        y = y * rest.pop(0)[...].astype(F32)
    o_ref[...] = y.astype(o_ref.dtype)


def _conv3(rg, conv_ins, w, width, name, bias=None, use_silu=False, post=None, tc_pref=1024):
    tc = _tile(width, tc_pref)
    tm = ROW_TILE
    hpt = tm // BF16_ROWS
    last_halo = rg.rows // BF16_ROWS - 1
    in_specs, args = [], []
    for arr, off in conv_ins:
        assert off % tc == 0 and arr.dtype == BF16
        ob = off // tc
        in_specs += [
            pl.BlockSpec((tm, tc), lambda i, j, ob=ob: (i, ob + j)),
            pl.BlockSpec((BF16_ROWS, tc), lambda i, j, ob=ob: (jnp.maximum(i * hpt - 1, 0), ob + j)),
            pl.BlockSpec((BF16_ROWS, tc), lambda i, j, ob=ob: (jnp.minimum((i + 1) * hpt, last_halo), ob + j)),
        ]
        args += [arr, arr, arr]
    in_specs.append(pl.BlockSpec((CONV_K, tc), lambda i, j: (0, j)))
    args.append(w)
    if bias is not None:
        in_specs.append(pl.BlockSpec((1, tc), lambda i, j: (0, j)))
        args.append(bias.reshape(1, width))
    if post is not None:
        arr, off = post
        assert off % tc == 0
        ob = off // tc
        in_specs.append(pl.BlockSpec((tm, tc), lambda i, j, ob=ob: (i, ob + j)))
        args.append(arr)
    blocks = (len(conv_ins) + 2) * _nbytes((tm, tc), BF16) + 6 * _nbytes((tm, tc), F32)
    return pl.pallas_call(
        functools.partial(_conv_kernel, len(conv_ins), bias is not None, use_silu, post is not None, rg.tpb, rg.ct),
        grid=(rg.ntiles, width // tc),
        in_specs=in_specs,
        out_specs=pl.BlockSpec((tm, tc), lambda i, j: (i, j)),
        out_shape=jax.ShapeDtypeStruct((rg.rows, width), BF16),
        compiler_params=_params(("parallel", "parallel"), blocks),
        name=name,
    )(*args)


def _cumsum_rows(v, reverse):
    n = v.shape[0]
    rows = lax.broadcasted_iota(jnp.int32, v.shape, 0)
    shift = 1
    while shift < n:
        if reverse:
            v = v + jnp.where(rows < n - shift, pltpu.roll(v, n - shift, 0), 0.0)
        else:
            v = v + jnp.where(rows >= shift, pltpu.roll(v, shift, 0), 0.0)
        shift *= 2
    return v


def _rope(blk, cos, sin):
    return blk * cos + pltpu.roll(blk, QK_ROPE, 1) * sin


def _latprep_kernel(nheads, lat_ref, gq_ref, gkv_ref, cos_ref, sin_ref, dtb_ref, alog_ref,
                    cq_ref, ckv_ref, kpe_ref, arow_ref, dtrow_ref):
    lat = lat_ref[...]
    cq_ref[...] = (_rms(lat[:, 0:Q_LORA]) * gq_ref[...]).astype(cq_ref.dtype)
    ckv_ref[...] = (_rms(lat[:, Q_LORA:Q_LORA + KV_LORA]) * gkv_ref[...]).astype(ckv_ref.dtype)
    o = Q_LORA + KV_LORA
    kpe_ref[...] = _rope(lat[:, o:o + LANES], cos_ref[...], sin_ref[...]).astype(kpe_ref.dtype)
    raw = lat[:, o + LANES:o + 2 * LANES] + dtb_ref[...]
    dt = jnp.maximum(raw, 0.0) + jnp.log1p(jnp.exp(-jnp.abs(raw)))
    adt = dt * (-jnp.exp(alog_ref[...]))
    lanes = lax.broadcasted_iota(jnp.int32, (SSM_CHUNK, LANES), 1)
    for c in range(lat.shape[0] // SSM_CHUNK):
        blk = adt[c * SSM_CHUNK:(c + 1) * SSM_CHUNK, :]
        acc = jnp.where(lanes < nheads, _cumsum_rows(blk, False), _cumsum_rows(blk, True))
        arow_ref[:, c * SSM_CHUNK:(c + 1) * SSM_CHUNK] = acc.T
        dtrow_ref[:, c * SSM_CHUNK:(c + 1) * SSM_CHUNK] = dt[c * SSM_CHUNK:(c + 1) * SSM_CHUNK, :].T


def _latprep(rg, lat, g_qa, g_kva, cos_t, sin_t, dt_bias, a_log, nheads):
    tm = ROW_TILE
    wlat = lat.shape[1]
    assert 2 * nheads == LANES and wlat == Q_LORA + KV_LORA + 2 * LANES
    vec = lambda n: pl.BlockSpec((1, n), lambda i: (0, 0))
    tab = pl.BlockSpec((tm, LANES), lambda i: (i % rg.tpb, 0))
    rowsp = lambda n: pl.BlockSpec((tm, n), lambda i: (i, 0))
    colsp = pl.BlockSpec((LANES, tm), lambda i: (0, i))
    return pl.pallas_call(
        functools.partial(_latprep_kernel, nheads),
        grid=(rg.ntiles,),
        in_specs=[rowsp(wlat), vec(Q_LORA), vec(KV_LORA), tab, tab, vec(LANES), vec(LANES)],
        out_specs=[rowsp(Q_LORA), rowsp(KV_LORA), rowsp(LANES), colsp, colsp],
        out_shape=[jax.ShapeDtypeStruct((rg.rows, Q_LORA), BF16),
                   jax.ShapeDtypeStruct((rg.rows, KV_LORA), BF16),
                   jax.ShapeDtypeStruct((rg.rows, LANES), BF16),
                   jax.ShapeDtypeStruct((LANES, rg.rows), F32),
                   jax.ShapeDtypeStruct((LANES, rg.rows), F32)],
        compiler_params=_params(("parallel",), 4 * _nbytes((tm, wlat), F32)),
        name="latent_prep",
    )(lat, g_qa.reshape(1, -1), g_kva.reshape(1, -1), cos_t, sin_t, dt_bias, a_log)


def _qproj_kernel(nh, qscale, cq_ref, w_ref, cos_ref, sin_ref, q_ref):
    acc = jnp.dot(cq_ref[...], w_ref[...], preferred_element_type=F32) * qscale
    cos, sin = cos_ref[...], sin_ref[...]
    for h in range(nh):
        q_ref[h, :, 0:QK_NOPE] = acc[:, h * QK_NOPE:(h + 1) * QK_NOPE].astype(q_ref.dtype)
        blk = acc[:, nh * QK_NOPE + h * LANES:nh * QK_NOPE + (h + 1) * LANES]
        q_ref[h, :, QK_NOPE:QK_DIM] = _rope(blk, cos, sin)[:, 0:QK_ROPE].astype(q_ref.dtype)


def _kvproj_kernel(nh, ckv_ref, w_ref, kpe_ref, k_ref, v_ref):
    acc = jnp.dot(ckv_ref[...], w_ref[...], preferred_element_type=F32)
    kpe = kpe_ref[...][:, 0:QK_ROPE]
    hw = QK_NOPE + V_DIM
    for h in range(nh):
        k_ref[h, :, 0:QK_NOPE] = acc[:, h * hw:h * hw + QK_NOPE].astype(k_ref.dtype)
        k_ref[h, :, QK_NOPE:QK_DIM] = kpe
        v_ref[h] = acc[:, h * hw + QK_NOPE:(h + 1) * hw].astype(v_ref.dtype)


def _qproj(rg, cq, wq, cos_t, sin_t, nh, qscale):
    tm = ROW_TILE
    blocks = _nbytes(wq.shape, BF16) + _nbytes((tm, wq.shape[1]), F32) + _nbytes((nh, tm, 2 * LANES), BF16)
    return pl.pallas_call(
        functools.partial(_qproj_kernel, nh, qscale),
        grid=(rg.nb, rg.tpb),
        in_specs=[pl.BlockSpec((tm, Q_LORA), lambda b, t: (b * rg.tpb + t, 0)),
                  pl.BlockSpec(wq.shape, lambda b, t: (0, 0)),
                  pl.BlockSpec((tm, LANES), lambda b, t: (t, 0)),
                  pl.BlockSpec((tm, LANES), lambda b, t: (t, 0))],
        out_specs=pl.BlockSpec((None, nh, tm, QK_DIM), lambda b, t: (b, 0, t, 0)),
        out_shape=jax.ShapeDtypeStruct((rg.nb, nh, rg.lseq, QK_DIM), BF16),
        compiler_params=_params(("parallel", "parallel"), blocks),
        name="q_proj",
    )(cq, wq, cos_t, sin_t)


def _kvproj(rg, ckv, wkv, kpe, nh):
    tm = ROW_TILE
    blocks = _nbytes(wkv.shape, BF16) + _nbytes((tm, wkv.shape[1]), F32) + 2 * _nbytes((nh, tm, 2 * LANES), BF16)
    return pl.pallas_call(
        functools.partial(_kvproj_kernel, nh),
        grid=(rg.nb, rg.tpb),
        in_specs=[pl.BlockSpec((tm, KV_LORA), lambda b, t: (b * rg.tpb + t, 0)),
                  pl.BlockSpec(wkv.shape, lambda b, t: (0, 0)),
                  pl.BlockSpec((tm, LANES), lambda b, t: (b * rg.tpb + t, 0))],
        out_specs=[pl.BlockSpec((None, nh, tm, QK_DIM), lambda b, t: (b, 0, t, 0)),
                   pl.BlockSpec((None, nh, tm, V_DIM), lambda b, t: (b, 0, t, 0))],
        out_shape=[jax.ShapeDtypeStruct((rg.nb, nh, rg.lseq, QK_DIM), BF16),
                   jax.ShapeDtypeStruct((rg.nb, nh, rg.lseq, V_DIM), BF16)],
        compiler_params=_params(("parallel", "parallel"), blocks),
        name="kv_proj",
    )(ckv, wkv, kpe)


def _attn_kernel(ctx, tk_pref, q_ref, k_ref, v_ref, o_ref, m_ref, l_ref, acc_ref):
    tq = q_ref.shape[0]
    lseq = k_ref.shape[0]

    def attend(r0, r1, kvlen):
        tk = _tile(kvlen, tk_pref)
        q = q_ref[r0:r1, :]
        m_ref[r0:r1, :] = jnp.full((r1 - r0, LANES), -jnp.inf, F32)
        l_ref[r0:r1, :] = jnp.zeros((r1 - r0, LANES), F32)
        acc_ref[r0:r1, :] = jnp.zeros((r1 - r0, V_DIM), F32)

        def body(j, carry):
            off = pl.multiple_of(j * tk, tk)
            k = k_ref[pl.ds(off, tk), :]
            v = v_ref[pl.ds(off, tk), :]
            s = lax.dot_general(q, k, (((1,), (1,)), ((), ())), preferred_element_type=F32)
            m_prev = m_ref[r0:r1, :]
            m_new = jnp.maximum(m_prev, jnp.max(s, axis=1, keepdims=True))
            alpha = jnp.exp2(m_prev - m_new)
            p = jnp.exp2(s - jnp.concatenate([m_new] * (tk // LANES), axis=1))
            l_ref[r0:r1, :] = alpha * l_ref[r0:r1, :] + jnp.sum(p, axis=1, keepdims=True)
            acc_ref[r0:r1, :] = alpha * acc_ref[r0:r1, :] + jnp.dot(p.astype(v.dtype), v,
                                                                    preferred_element_type=F32)
            m_ref[r0:r1, :] = m_new
            return carry

        lax.fori_loop(0, kvlen // tk, body, 0)
        o_ref[r0:r1, :] = (acc_ref[r0:r1, :] / l_ref[r0:r1, :]).astype(o_ref.dtype)

    first = pl.program_id(2) == 0

    @pl.when(first)
    def _():
        attend(0, ctx, ctx)
        if tq > ctx:
            attend(ctx, tq, lseq)

    @pl.when(jnp.logical_not(first))
    def _():
        attend(0, tq, lseq)


def _attention(rg, q, k, v, nh, tq_pref=768, tk_pref=512):
    lseq = rg.lseq
    tq = _tile(lseq, tq_pref, ROW_TILE)
    assert tq >= rg.ctx and V_DIM == LANES
    nq = lseq // tq
    blocks = _nbytes((lseq, 2 * LANES), BF16) + _nbytes((lseq, V_DIM), BF16) + 8 * _nbytes((tq, tk_pref), F32) // 2
    return pl.pallas_call(
        functools.partial(_attn_kernel, rg.ctx, tk_pref),
        grid=(rg.nb, nh, nq),
        in_specs=[pl.BlockSpec((None, None, tq, QK_DIM), lambda b, h, i: (b, h, i, 0)),
                  pl.BlockSpec((None, None, lseq, QK_DIM), lambda b, h, i: (b, h, 0, 0)),
                  pl.BlockSpec((None, None, lseq, V_DIM), lambda b, h, i: (b, h, 0, 0))],
        out_specs=pl.BlockSpec((tq, V_DIM), lambda b, h, i: (b * nq + i, h)),
        out_shape=jax.ShapeDtypeStruct((rg.rows, nh * V_DIM), BF16),
        scratch_shapes=[pltpu.VMEM((tq, LANES), F32), pltpu.VMEM((tq, LANES), F32), pltpu.VMEM((tq, V_DIM), F32)],
        compiler_params=_params(("parallel", "parallel", "arbitrary"), blocks),
        name="mla_attention",
    )(q, k, v)


def _ssd_kernel(reverse, ngroups, x_ref, b_ref, c_ref, a_ref, dt_ref, extra_ref, y_ref, s_ref):
    q = SSM_CHUNK
    hpg = a_ref.shape[0] // ngroups
    pair_w = 2 * SSM_HEADDIM
    assert pair_w == LANES and SSM_STATE == LANES and hpg % 2 == 0
    gw = hpg * SSM_HEADDIM

    @pl.when(pl.program_id(1) == 0)
    def _():
        s_ref[...] = jnp.zeros(s_ref.shape, F32)

    rows = lax.broadcasted_iota(jnp.int32, (q, q), 0)
    cols = lax.broadcasted_iota(jnp.int32, (q, q), 1)
    causal = (rows <= cols) if reverse else (rows >= cols)
    low = cols < SSM_HEADDIM
    last = 0 if reverse else q - 1

    def group(g, carry):
        goff = pl.multiple_of(g * SSM_STATE, SSM_STATE)
        bg = b_ref[:, pl.ds(goff, SSM_STATE)]
        cg = c_ref[:, pl.ds(goff, SSM_STATE)]
        cb = lax.dot_general(cg, bg, (((1,), (1,)), ((), ())), preferred_element_type=F32)
        bt = bg.astype(F32).T
        hoff = pl.multiple_of(g * hpg, hpg)
        ag = a_ref[pl.ds(hoff, hpg), :]
        dtg = dt_ref[pl.ds(hoff, hpg), :]
        sg = s_ref[g]
        y_off = jnp.dot(cg, sg.astype(BF16), preferred_element_type=F32)
        xoff = pl.multiple_of(g * gw, gw)
        for k in range(hpg // 2):
            top, bot, e_col, e_tot = [], [], [], []
            for hh in (2 * k, 2 * k + 1):
                a_r = ag[hh:hh + 1, :]
                dt_r = dtg[hh:hh + 1, :]
                a_lane = jnp.broadcast_to(a_r, (q, q))
                a_sub = a_lane.T
                decay = jnp.exp(jnp.where(causal, a_sub - a_lane, -jnp.inf))
                top.append(cb * decay * dt_r)
                tot = a_r[:, last:last + 1]
                bot.append(bt * (jnp.exp(tot - a_r) * dt_r))
                e_col.append(jnp.exp(a_sub))
                e_tot.append(jnp.exp(tot))
            lhs = jnp.concatenate([jnp.concatenate(top, axis=1), jnp.concatenate(bot, axis=1)], axis=0)
            xp = x_ref[:, pl.ds(xoff + k * pair_w, pair_w)].astype(F32)
            rhs = jnp.concatenate([jnp.where(low, xp, 0.0), jnp.where(low, 0.0, xp)], axis=0)
            r = jnp.dot(lhs.astype(BF16), rhs.astype(BF16), preferred_element_type=F32)
            lo, hi = k * pair_w, (k + 1) * pair_w
            y = r[0:q, :] + y_off[:, lo:hi] * jnp.where(low, e_col[0], e_col[1])
            s_ref[g, :, lo:hi] = jnp.where(low, e_tot[0], e_tot[1]) * sg[:, lo:hi] + r[q:2 * q, :]
            ext = extra_ref[:, pl.ds(xoff + k * pair_w, pair_w)]
            y = y + (ext if reverse else ext * xp)
            y_ref[:, pl.ds(xoff + k * pair_w, pair_w)] = y.astype(y_ref.dtype)
        return carry

    lax.fori_loop(0, ngroups, group, 0)


def _ssd(rg, xbc, arow, dtrow, extra, reverse, nheads, out_dtype):
    q = SSM_CHUNK
    inner = nheads * SSM_HEADDIM
    gn = SSM_GROUPS * SSM_STATE
    assert inner % gn == 0 and xbc.shape[1] == inner + 2 * gn
    nch = rg.lseq // q
    ncc = rg.ctx // q

    def chunk(b, s):
        c = jnp.where(s < ncc, ncc - 1 - s, nch - 1 - (s - ncc)) if reverse else s
        return b * nch + c

    d = 1 if reverse else 0
    extra_spec = (pl.BlockSpec((q, inner), lambda b, s: (chunk(b, s), 0)) if reverse
                  else pl.BlockSpec((1, inner), lambda b, s: (0, 0)))
    blocks = 3 * _nbytes((q, inner), F32) + _nbytes((SSM_GROUPS, SSM_STATE, inner // SSM_GROUPS), F32)
    return pl.pallas_call(
        functools.partial(_ssd_kernel, reverse, SSM_GROUPS),
        grid=(rg.nb, nch),
        in_specs=[pl.BlockSpec((q, inner), lambda b, s: (chunk(b, s), 0)),
                  pl.BlockSpec((q, gn), lambda b, s: (chunk(b, s), inner // gn)),
                  pl.BlockSpec((q, gn), lambda b, s: (chunk(b, s), inner // gn + 1)),
                  pl.BlockSpec((nheads, q), lambda b, s: (d, chunk(b, s))),
                  pl.BlockSpec((nheads, q), lambda b, s: (d, chunk(b, s))),
                  extra_spec],
        out_specs=pl.BlockSpec((q, inner), lambda b, s: (chunk(b, s), 0)),
        out_shape=jax.ShapeDtypeStruct((rg.rows, inner), out_dtype),
        scratch_shapes=[pltpu.VMEM((SSM_GROUPS, SSM_STATE, inner // SSM_GROUPS), F32)],
        compiler_params=_params(("parallel", "arbitrary"), blocks),
        name="ssd_scan_bwd" if reverse else "ssd_scan_fwd",
    )(xbc, xbc, xbc, arow, dtrow, extra)


def _ssmgate_kernel(y_ref, z_ref, g_ref, o_ref):
    v = y_ref[...].astype(F32) * _silu(z_ref[...].astype(F32))
    o_ref[...] = (_rms(v) * g_ref[...]).astype(o_ref.dtype)


def _ssmgate(rg, y, hz, g):
    inner = y.shape[1]
    row = lambda i: (i, 0)
    return pl.pallas_call(
        _ssmgate_kernel,
        grid=(rg.ntiles,),
        in_specs=[pl.BlockSpec((ROW_TILE, inner), row), pl.BlockSpec((ROW_TILE, inner), row),
                  pl.BlockSpec((1, inner), lambda i: (0, 0))],
        out_specs=pl.BlockSpec((ROW_TILE, inner), row),
        out_shape=jax.ShapeDtypeStruct(y.shape, BF16),
        compiler_params=_params(("parallel",), 5 * _nbytes((ROW_TILE, inner), F32)),
        name="ssm_gate_norm",
    )(y, hz, g.reshape(1, inner))


def _merge_kernel(g0_ref, g1_ref, g2_ref, ya_ref, yb_ref, yc_ref, o_ref):
    def term(g_ref, y_ref):
        return jax.nn.sigmoid(g_ref[...].astype(F32)) * y_ref[...].astype(F32)

    o_ref[...] = (term(g0_ref, ya_ref) + term(g1_ref, yb_ref) + term(g2_ref, yc_ref)).astype(o_ref.dtype)


def _merge(rg, gates, ya, yb, yc):
    d = ya.shape[1]
    tc = _tile(d, 1024)
    nb = d // tc
    gspec = lambda k: pl.BlockSpec((ROW_TILE, tc), lambda i, j, k=k: (i, k * nb + j))
    yspec = pl.BlockSpec((ROW_TILE, tc), lambda i, j: (i, j))
    return pl.pallas_call(
        _merge_kernel,
        grid=(rg.ntiles, nb),
        in_specs=[gspec(0), gspec(1), gspec(2), yspec, yspec, yspec],
        out_specs=yspec,
        out_shape=jax.ShapeDtypeStruct(ya.shape, BF16),
        compiler_params=_params(("parallel", "parallel"), 8 * _nbytes((ROW_TILE, tc), F32)),
        name="merge_branches",
    )(gates, gates, gates, ya, yb, yc)


def _rope_tables(ctx, seq):
    nf = QK_ROPE // 4
    pos = jnp.arange(seq)
    row = (pos // GRID_W).astype(F32)
    col = (pos % GRID_W).astype(F32)
    inv = ROPE_BASE ** (-jnp.arange(nf, dtype=F32) / nf)
    ang = jnp.stack([row[:, None] * inv, col[:, None] * inv], axis=1)
    cos = jnp.cos(ang)
    sin = jnp.sin(ang)
    cos_l = jnp.stack([cos, cos], axis=2).reshape(seq, QK_ROPE)
    sin_l = jnp.stack([-sin, sin], axis=2).reshape(seq, QK_ROPE)
    pad = jnp.zeros((seq, LANES - QK_ROPE), F32)
    cos_x = jnp.concatenate([cos_l, pad], axis=1)
    sin_x = jnp.concatenate([sin_l, pad], axis=1)
    cos_c = jnp.concatenate([jnp.ones((ctx, QK_ROPE), F32), jnp.zeros((ctx, LANES - QK_ROPE), F32)], axis=1)
    sin_c = jnp.zeros((ctx, LANES), F32)
    return jnp.concatenate([cos_c, cos_x], axis=0), jnp.concatenate([sin_c, sin_x], axis=0)


def _half_swap_perm():
    nf = QK_ROPE // 4
    idx = np.arange(QK_ROPE).reshape(2, 2, nf)
    return idx[:, ::-1, :].reshape(-1)


def kernel(x, c, ctx, c_ctx, w_ada, b_ada, w_in, w_sc_conv, w_sc_out, g_qa, w_uq, g_kva, w_ukv, w_mla_out,
           w_ssm_conv, b_ssm_conv, dt_bias_f, dt_bias_b, a_log_f, a_log_b, d_skip, g_ssm_norm, w_ssm_out, w_out,
           ln1_g, ln1_b, w_up, w_ff_conv, w_down, ln2_g, ln2_b):
    n_batch, seq, d = x.shape
    n_ctx = ctx.shape[1]
    depth = w_in.shape[0]
    nh = MLA_HEADS
    inner = w_ssm_out.shape[1]
    nheads = inner // SSM_HEADDIM
    gn = SSM_GROUPS * SSM_STATE
    d_ff = w_down.shape[1]
    alpha = (2 * depth) ** 0.25
    rg = _Rows(n_batch, n_ctx, seq)
    assert n_batch < 8

    splits = (d, d, d, Q_LORA, KV_LORA, QK_ROPE, inner, inner, gn, gn, 2 * nheads, N_BRANCH * d)
    o = np.concatenate([[0], np.cumsum(splits)])
    assert o[-1] == w_in.shape[2]
    swap = _half_swap_perm()

    cos_t, sin_t = _rope_tables(n_ctx, seq)

    cond8 = jnp.zeros((8, d), F32).at[0:n_batch].set(c).at[n_batch].set(c_ctx)
    mod_all = _ada(cond8, w_ada, b_ada)
    mod_all = mod_all.reshape(depth, 8, 6, d).transpose(0, 2, 1, 3)[:, :, :, None, :]

    h = jnp.concatenate([ctx, x], axis=1).reshape(rg.rows, d)
    u = _lnmod(rg, h, mod_all[0], 0, 1)

    for l in range(depth):
        mod = mod_all[l]
        wl = w_in[l]
        w_a = wl[:, o[0]:o[3]].astype(BF16)
        w_b = wl[:, o[6]:o[10]].astype(BF16)
        w_c = wl[:, o[11]:o[12]].astype(BF16)
        kr = wl[:, o[5]:o[6]]
        w_d = jnp.concatenate([wl[:, o[3]:o[5]], kr, kr[:, swap], wl[:, o[10]:o[11]]], axis=1).astype(BF16)
        wq3 = w_uq[l].reshape(Q_LORA, nh, QK_DIM)
        pe = wq3[:, :, QK_NOPE:]
        wq = jnp.concatenate([wq3[:, :, :QK_NOPE].reshape(Q_LORA, nh * QK_NOPE),
                              jnp.concatenate([pe, pe[:, :, swap]], axis=2).reshape(Q_LORA, nh * LANES)],
                             axis=1).astype(BF16)
        wkv = w_ukv[l].astype(BF16)

        h_a = _matmul(u, w_a, BF16, "in_proj_conv")
        h_b = _matmul(u, w_b, BF16, "in_proj_ssm")
        h_c = _matmul(u, w_c, BF16, "in_proj_gates")
        lat = _matmul(u, w_d, F32, "in_proj_latent", tn_pref=1280)

        ga = _conv3(rg, [(h_a, d), (h_a, 2 * d)], w_sc_conv[l], d, "sconv_mix", post=(h_a, 0))
        ya = _matmul(ga, w_sc_out[l].astype(BF16), BF16, "sconv_out")

        dt_bias = jnp.concatenate([dt_bias_f[l], dt_bias_b[l]]).reshape(1, 2 * nheads)
        a_log = jnp.concatenate([a_log_f[l], a_log_b[l]]).reshape(1, 2 * nheads)
        cq, ckv, kpe, arow, dtrow = _latprep(rg, lat, g_qa[l], g_kva[l], cos_t, sin_t, dt_bias, a_log, nheads)
        q = _qproj(rg, cq, wq, cos_t, sin_t, nh, (QK_DIM ** -0.5) * LOG2E)
        k, v = _kvproj(rg, ckv, wkv, kpe, nh)
        att = _attention(rg, q, k, v, nh)
        yb = _matmul(att, w_mla_out[l].astype(BF16), BF16, "mla_out")

        xbc = _conv3(rg, [(h_b, inner)], w_ssm_conv[l], inner + 2 * gn, "ssm_conv", bias=b_ssm_conv[l],
                     use_silu=True)
        skip = jnp.repeat(d_skip[l], SSM_HEADDIM).reshape(1, inner)
        y_f = _ssd(rg, xbc, arow, dtrow, skip, False, nheads, F32)
        y_s = _ssd(rg, xbc, arow, dtrow, y_f, True, nheads, BF16)
        gc = _ssmgate(rg, y_s, h_b, g_ssm_norm[l])
        yc = _matmul(gc, w_ssm_out[l].astype(BF16), BF16, "ssm_out")

        merged = _merge(rg, h_c, ya, yb, yc)
        y1 = _matmul(merged, w_out[l].astype(BF16), F32, "mix_out")
        h, u2 = _postnorm(rg, alpha, h, y1, mod, 2, ln1_g[l], ln1_b[l], mod, 3, 4)

        h_f = _matmul(u2, w_up[l].astype(BF16), BF16, "ffn_up")
        gf = _conv3(rg, [(h_f, 0)], w_ff_conv[l], d_ff, "ffn_conv", use_silu=True, post=(h_f, d_ff), tc_pref=512)
        y2 = _matmul(gf, w_down[l].astype(BF16), F32, "ffn_down", tn_pref=512)
        if l + 1 < depth:
            h, u = _postnorm(rg, alpha, h, y2, mod, 5, ln2_g[l], ln2_b[l], mod_all[l + 1], 0, 1)
        else:
            h, _ = _postnorm(rg, alpha, h, y2, mod, 5, ln2_g[l], ln2_b[l])

    return h.reshape(n_batch, rg.lseq, d)[:, n_ctx:, :]
```

```python
import functools
import math

import numpy as np
import jax
import jax.numpy as jnp
from jax import lax
from jax.experimental import pallas as pl
from jax.experimental.pallas import tpu as pltpu

GRID_W = 64
CONV_K = 3
MLA_HEADS = 16
Q_LORA = 512
KV_LORA = 512
QK_NOPE = 128
QK_ROPE = 64
V_DIM = 128
ROPE_BASE = 10000.0
SSM_HEADDIM = 64
SSM_GROUPS = 8
SSM_STATE = 128
SSM_CHUNK = 128
N_BRANCH = 3
EPS = 1e-6

LANES = 128
BF16_ROWS = 16
VMEM_CAP_BYTES = 56 * 2 ** 20
ROW_TILE = 256

F32 = jnp.float32
BF16 = jnp.bfloat16
QK_DIM = QK_NOPE + QK_ROPE
LOG2E = math.log2(math.e)


def _params(sem, block_bytes):
    limit = int(min(VMEM_CAP_BYTES, 2 * block_bytes + 24 * 2 ** 20))
    return pltpu.CompilerParams(dimension_semantics=sem, vmem_limit_bytes=limit)


def _nbytes(shape, dtype):
    return int(np.prod(shape)) * jnp.dtype(dtype).itemsize


def _tile(n, pref, unit=LANES):
    if n <= pref:
        return n
    best = None
    for t in range(unit, pref + 1, unit):
        if n % t == 0:
            best = t
    assert best is not None, (n, pref)
    return best


def _silu(v):
    return v * jax.nn.sigmoid(v)


def _layer_norm(v):
    mu = jnp.mean(v, axis=-1, keepdims=True)
    vc = v - mu
    var = jnp.mean(vc * vc, axis=-1, keepdims=True)
    return vc * lax.rsqrt(var + EPS)


def _rms(v):
    return v * lax.rsqrt(jnp.mean(v * v, axis=-1, keepdims=True) + EPS)


def _ada_kernel(c_ref, w_ref, b_ref, o_ref):
    s = _silu(c_ref[...])
    o_ref[...] = jnp.dot(s.astype(BF16), w_ref[...].astype(BF16), preferred_element_type=F32) + b_ref[...]


def _ada(cond8, w_ada, b_ada):
    depth, d, n = w_ada.shape
    tn = _tile(n, 1024)
    blocks = _nbytes((d, tn), F32) + _nbytes((8, d), F32) + _nbytes((8, tn), F32)
    return pl.pallas_call(
        _ada_kernel,
        grid=(depth, n // tn),
        in_specs=[pl.BlockSpec((8, d), lambda l, j: (0, 0)),
                  pl.BlockSpec((None, d, tn), lambda l, j: (l, 0, j)),
                  pl.BlockSpec((None, 1, tn), lambda l, j: (l, 0, j))],
        out_specs=pl.BlockSpec((None, 8, tn), lambda l, j: (l, 0, j)),
        out_shape=jax.ShapeDtypeStruct((depth, 8, n), F32),
        compiler_params=_params(("parallel", "parallel"), blocks),
        name="ada_mod",
    )(cond8, w_ada, b_ada.reshape(depth, 1, n))


class _Rows:
    def __init__(self, n_batch, ctx, seq):
        self.nb, self.ctx, self.seq = n_batch, ctx, seq
        self.lseq = ctx + seq
        self.rows = n_batch * self.lseq
        assert ctx % ROW_TILE == 0 and seq % ROW_TILE == 0
        self.tpb = self.lseq // ROW_TILE
        self.ct = ctx // ROW_TILE
        self.ntiles = self.rows // ROW_TILE

    def seg(self, i):
        return jnp.where(i % self.tpb < self.ct, self.nb, i // self.tpb)

    def mod_spec(self, k, d):
        return pl.BlockSpec((None, None, 1, d), lambda i: (k, self.seg(i), 0, 0))


def _lnmod_kernel(x_ref, sh_ref, sc_ref, u_ref):
    u_ref[...] = (_layer_norm(x_ref[...]) * (1.0 + sc_ref[...]) + sh_ref[...]).astype(u_ref.dtype)


def _lnmod(rg, x, mod, k_shift, k_scale):
    d = x.shape[1]
    row = pl.BlockSpec((ROW_TILE, d), lambda i: (i, 0))
    return pl.pallas_call(
        _lnmod_kernel,
        grid=(rg.ntiles,),
        in_specs=[row, rg.mod_spec(k_shift, d), rg.mod_spec(k_scale, d)],
        out_specs=row,
        out_shape=jax.ShapeDtypeStruct(x.shape, BF16),
        compiler_params=_params(("parallel",), 2 * _nbytes((ROW_TILE, d), F32)),
        name="ln_modulate",
    )(x, mod, mod)


def _postnorm_kernel(alpha, with_next, x_ref, y_ref, gate_ref, g_ref, b_ref, *rest):
    if with_next:
        sh_ref, sc_ref, xo_ref, u_ref = rest
    else:
        (xo_ref,) = rest
    v = alpha * x_ref[...] + gate_ref[...] * y_ref[...]
    xn = _layer_norm(v) * g_ref[...] + b_ref[...]
    xo_ref[...] = xn
    if with_next:
        u_ref[...] = (_layer_norm(xn) * (1.0 + sc_ref[...]) + sh_ref[...]).astype(u_ref.dtype)


def _postnorm(rg, alpha, x, y, mod, k_gate, ln_g, ln_b, next_mod=None, k_shift=None, k_scale=None):
    d = x.shape[1]
    row = pl.BlockSpec((ROW_TILE, d), lambda i: (i, 0))
    vec = pl.BlockSpec((1, d), lambda i: (0, 0))
    with_next = next_mod is not None
    in_specs = [row, row, rg.mod_spec(k_gate, d), vec, vec]
    args = [x, y, mod, ln_g.reshape(1, d), ln_b.reshape(1, d)]
    out_specs = [row]
    out_shape = [jax.ShapeDtypeStruct(x.shape, F32)]
    if with_next:
        in_specs += [rg.mod_spec(k_shift, d), rg.mod_spec(k_scale, d)]
        args += [next_mod, next_mod]
        out_specs.append(row)
        out_shape.append(jax.ShapeDtypeStruct(x.shape, BF16))
    out = pl.pallas_call(
        functools.partial(_postnorm_kernel, alpha, with_next),
        grid=(rg.ntiles,),
        in_specs=in_specs,
        out_specs=out_specs,
        out_shape=out_shape,
        compiler_params=_params(("parallel",), 4 * _nbytes((ROW_TILE, d), F32)),
        name="post_norm",
    )(*args)
    return (out[0], out[1]) if with_next else (out[0], None)


def _mm_kernel(a_ref, w_ref, o_ref):
    o_ref[...] = jnp.dot(a_ref[...], w_ref[...], preferred_element_type=F32).astype(o_ref.dtype)


def _matmul(a, w, out_dtype, name, tm_pref=768, tn_pref=1024):
    m, k = a.shape
    n = w.shape[1]
    tm = _tile(m, tm_pref, 8)
    tn = _tile(n, tn_pref)
    blocks = _nbytes((tm, k), BF16) + _nbytes((k, tn), BF16) + _nbytes((tm, tn), F32)
    return pl.pallas_call(
        _mm_kernel,
        grid=(m // tm, n // tn),
        in_specs=[pl.BlockSpec((tm, k), lambda i, j: (i, 0)),
                  pl.BlockSpec((k, tn), lambda i, j: (0, j))],
        out_specs=pl.BlockSpec((tm, tn), lambda i, j: (i, j)),
        out_shape=jax.ShapeDtypeStruct((m, n), out_dtype),
        compiler_params=_params(("parallel", "parallel"), blocks),
        name=name,
    )(a, w)


def _conv_kernel(n_in, has_bias, use_silu, has_post, tpb, ct, *refs):
    refs = list(refs)
    o_ref = refs.pop()
    i = pl.program_id(0)
    t = i % tpb
    starts_seq = (t == 0) | (t == ct)
    ends_seq = (t == ct - 1) | (t == tpb - 1)
    v = vp = vn = None
    for k in range(n_in):
        main = refs[3 * k][...].astype(F32)
        prev = refs[3 * k + 1][...].astype(F32)[BF16_ROWS - 1:BF16_ROWS, :]
        nxt = refs[3 * k + 2][...].astype(F32)[0:1, :]
        v = main if v is None else v * main
        vp = prev if vp is None else vp * prev
        vn = nxt if vn is None else vn * nxt
    rest = refs[3 * n_in:]
    w = rest.pop(0)[...]
    vp = jnp.where(starts_seq, 0.0, vp)
    vn = jnp.where(ends_seq, 0.0, vn)
    tm = v.shape[0]
    rows = lax.broadcasted_iota(jnp.int32, v.shape, 0)
    down = jnp.where(rows == 0, vp, pltpu.roll(v, 1, 0))
    up = jnp.where(rows == tm - 1, vn, pltpu.roll(v, tm - 1, 0))
    y = down * w[0:1, :] + v * w[1:2, :] + up * w[2:3, :]
    if has_bias:
        y = y + rest.pop(0)[...]
    if use_silu:
        y = _silu(y)
    if has_post:
        y = y * rest.pop(0)[...].astype(F32)
    o_ref[...] = y.astype(o_ref.dtype)


def _conv3(rg, conv_ins, w, width, name, bias=None, use_silu=False, post=None, tc_pref=3072):
    offsets = [off for _, off in conv_ins] + ([post[1]] if post is not None else [])
    tc = _tile(math.gcd(width, *offsets), tc_pref)
    tm = ROW_TILE
    hpt = tm // BF16_ROWS
    last_halo = rg.rows // BF16_ROWS - 1
    in_specs, args = [], []
    for arr, off in conv_ins:
        assert off % tc == 0 and arr.dtype == BF16
        ob = off // tc
        in_specs += [
            pl.BlockSpec((tm, tc), lambda i, j, ob=ob: (i, ob + j)),
            pl.BlockSpec((BF16_ROWS, tc), lambda i, j, ob=ob: (jnp.maximum(i * hpt - 1, 0), ob + j)),
            pl.BlockSpec((BF16_ROWS, tc), lambda i, j, ob=ob: (jnp.minimum((i + 1) * hpt, last_halo), ob + j)),
        ]
        args += [arr, arr, arr]
    in_specs.append(pl.BlockSpec((CONV_K, tc), lambda i, j: (0, j)))
    args.append(w)
    if bias is not None:
        in_specs.append(pl.BlockSpec((1, tc), lambda i, j: (0, j)))
        args.append(bias.reshape(1, width))
    if post is not None:
        arr, off = post
        assert off % tc == 0
        ob = off // tc
        in_specs.append(pl.BlockSpec((tm, tc), lambda i, j, ob=ob: (i, ob + j)))
        args.append(arr)
    blocks = (len(conv_ins) + 2) * _nbytes((tm, tc), BF16) + 6 * _nbytes((tm, tc), F32)
    return pl.pallas_call(
        functools.partial(_conv_kernel, len(conv_ins), bias is not None, use_silu, post is not None, rg.tpb, rg.ct),
        grid=(rg.ntiles, width // tc),
        in_specs=in_specs,
        out_specs=pl.BlockSpec((tm, tc), lambda i, j: (i, j)),
        out_shape=jax.ShapeDtypeStruct((rg.rows, width), BF16),
        compiler_params=_params(("parallel", "parallel"), blocks),
        name=name,
    )(*args)


def _cumsum_rows(v, reverse):
    n = v.shape[0]
    rows = lax.broadcasted_iota(jnp.int32, v.shape, 0)
    shift = 1
    while shift < n:
        if reverse:
            v = v + jnp.where(rows < n - shift, pltpu.roll(v, n - shift, 0), 0.0)
        else:
            v = v + jnp.where(rows >= shift, pltpu.roll(v, shift, 0), 0.0)
        shift *= 2
    return v


def _rope(blk, cos, sin):
    return blk * cos + pltpu.roll(blk, QK_ROPE, 1) * sin


def _latprep_kernel(nheads, lat_ref, gq_ref, gkv_ref, cos_ref, sin_ref, dtb_ref, alog_ref,
                    cq_ref, ckv_ref, kpe_ref, arow_ref, dtrow_ref):
    lat = lat_ref[...]
    cq_ref[...] = (_rms(lat[:, 0:Q_LORA]) * gq_ref[...]).astype(cq_ref.dtype)
    ckv_ref[...] = (_rms(lat[:, Q_LORA:Q_LORA + KV_LORA]) * gkv_ref[...]).astype(ckv_ref.dtype)
    o = Q_LORA + KV_LORA
    kpe_ref[...] = _rope(lat[:, o:o + LANES], cos_ref[...], sin_ref[...]).astype(kpe_ref.dtype)
    raw = lat[:, o + LANES:o + 2 * LANES] + dtb_ref[...]
    dt = jnp.maximum(raw, 0.0) + jnp.log1p(jnp.exp(-jnp.abs(raw)))
    adt = dt * (-jnp.exp(alog_ref[...]))
    lanes = lax.broadcasted_iota(jnp.int32, (SSM_CHUNK, LANES), 1)
    for c in range(lat.shape[0] // SSM_CHUNK):
        blk = adt[c * SSM_CHUNK:(c + 1) * SSM_CHUNK, :]
        acc = jnp.where(lanes < nheads, _cumsum_rows(blk, False), _cumsum_rows(blk, True))
        arow_ref[:, c * SSM_CHUNK:(c + 1) * SSM_CHUNK] = acc.T
        dtrow_ref[:, c * SSM_CHUNK:(c + 1) * SSM_CHUNK] = dt[c * SSM_CHUNK:(c + 1) * SSM_CHUNK, :].T


def _latprep(rg, lat, g_qa, g_kva, cos_t, sin_t, dt_bias, a_log, nheads):
    tm = ROW_TILE
    wlat = lat.shape[1]
    assert 2 * nheads == LANES and wlat == Q_LORA + KV_LORA + 2 * LANES
    vec = lambda n: pl.BlockSpec((1, n), lambda i: (0, 0))
    tab = pl.BlockSpec((tm, LANES), lambda i: (i % rg.tpb, 0))
    rowsp = lambda n: pl.BlockSpec((tm, n), lambda i: (i, 0))
    colsp = pl.BlockSpec((LANES, tm), lambda i: (0, i))
    return pl.pallas_call(
        functools.partial(_latprep_kernel, nheads),
        grid=(rg.ntiles,),
        in_specs=[rowsp(wlat), vec(Q_LORA), vec(KV_LORA), tab, tab, vec(LANES), vec(LANES)],
        out_specs=[rowsp(Q_LORA), rowsp(KV_LORA), rowsp(LANES), colsp, colsp],
        out_shape=[jax.ShapeDtypeStruct((rg.rows, Q_LORA), BF16),
                   jax.ShapeDtypeStruct((rg.rows, KV_LORA), BF16),
                   jax.ShapeDtypeStruct((rg.rows, LANES), BF16),
                   jax.ShapeDtypeStruct((LANES, rg.rows), F32),
                   jax.ShapeDtypeStruct((LANES, rg.rows), F32)],
        compiler_params=_params(("parallel",), 4 * _nbytes((tm, wlat), F32)),
        name="latent_prep",
    )(lat, g_qa.reshape(1, -1), g_kva.reshape(1, -1), cos_t, sin_t, dt_bias, a_log)


def _qproj_kernel(nh, qscale, cq_ref, w_ref, cos_ref, sin_ref, q_ref):
    acc = jnp.dot(cq_ref[...], w_ref[...], preferred_element_type=F32) * qscale
    cos, sin = cos_ref[...], sin_ref[...]
    for h in range(nh):
        q_ref[h, 0:QK_NOPE, :] = acc[:, h * QK_NOPE:(h + 1) * QK_NOPE].T.astype(q_ref.dtype)
        blk = acc[:, nh * QK_NOPE + h * LANES:nh * QK_NOPE + (h + 1) * LANES]
        q_ref[h, QK_NOPE:QK_DIM, :] = _rope(blk, cos, sin).T[0:QK_ROPE, :].astype(q_ref.dtype)


def _kvproj_kernel(nh, ckv_ref, w_ref, kpe_ref, k_ref, v_ref):
    acc = jnp.dot(ckv_ref[...], w_ref[...], preferred_element_type=F32)
    kpe = kpe_ref[...][:, 0:QK_ROPE]
    hw = QK_NOPE + V_DIM
    for h in range(nh):
        k_ref[h, :, 0:QK_NOPE] = acc[:, h * hw:h * hw + QK_NOPE].astype(k_ref.dtype)
        k_ref[h, :, QK_NOPE:QK_DIM] = kpe
        v_ref[h] = acc[:, h * hw + QK_NOPE:(h + 1) * hw].T.astype(v_ref.dtype)


def _qproj(rg, cq, wq, cos_t, sin_t, nh, qscale):
    tm = ROW_TILE
    blocks = _nbytes(wq.shape, BF16) + _nbytes((tm, wq.shape[1]), F32) + _nbytes((nh, tm, 2 * LANES), BF16)
    return pl.pallas_call(
        functools.partial(_qproj_kernel, nh, qscale),
        grid=(rg.nb, rg.tpb),
        in_specs=[pl.BlockSpec((tm, Q_LORA), lambda b, t: (b * rg.tpb + t, 0)),
                  pl.BlockSpec(wq.shape, lambda b, t: (0, 0)),
                  pl.BlockSpec((tm, LANES), lambda b, t: (t, 0)),
                  pl.BlockSpec((tm, LANES), lambda b, t: (t, 0))],
        out_specs=pl.BlockSpec((None, nh, QK_DIM, tm), lambda b, t: (b, 0, 0, t)),
        out_shape=jax.ShapeDtypeStruct((rg.nb, nh, QK_DIM, rg.lseq), BF16),
        compiler_params=_params(("parallel", "parallel"), blocks),
        name="q_proj",
    )(cq, wq, cos_t, sin_t)


def _kvproj(rg, ckv, wkv, kpe, nh):
    tm = ROW_TILE
    blocks = _nbytes(wkv.shape, BF16) + _nbytes((tm, wkv.shape[1]), F32) + 2 * _nbytes((nh, tm, 2 * LANES), BF16)
    return pl.pallas_call(
        functools.partial(_kvproj_kernel, nh),
        grid=(rg.nb, rg.tpb),
        in_specs=[pl.BlockSpec((tm, KV_LORA), lambda b, t: (b * rg.tpb + t, 0)),
                  pl.BlockSpec(wkv.shape, lambda b, t: (0, 0)),
                  pl.BlockSpec((tm, LANES), lambda b, t: (b * rg.tpb + t, 0))],
        out_specs=[pl.BlockSpec((None, nh, tm, QK_DIM), lambda b, t: (b, 0, t, 0)),
                   pl.BlockSpec((None, nh, V_DIM, tm), lambda b, t: (b, 0, 0, t))],
        out_shape=[jax.ShapeDtypeStruct((rg.nb, nh, rg.lseq, QK_DIM), BF16),
                   jax.ShapeDtypeStruct((rg.nb, nh, V_DIM, rg.lseq), BF16)],
        compiler_params=_params(("parallel", "parallel"), blocks),
        name="kv_proj",
    )(ckv, wkv, kpe)


ATTN_QBLOCK = 2 * LANES


def _attn_kernel(ctx, tk_pref, q_ref, k_ref, v_ref, o_ref, m_ref, l_ref, acc_ref, s_ref):
    tq = q_ref.shape[1]
    lseq = k_ref.shape[0]
    cb = ATTN_QBLOCK
    n_blocks = tq // cb
    n_ctx_blocks = ctx // cb

    def scores(c, k):
        return jnp.dot(k, q_ref[:, c * cb:(c + 1) * cb], preferred_element_type=F32)

    def update(c, s, vt):
        cols = slice(c * cb, (c + 1) * cb)
        m_prev = m_ref[:, cols]
        m_new = jnp.maximum(m_prev, jnp.max(s, axis=0, keepdims=True))
        alpha = jnp.exp2(m_prev - m_new)
        p = jnp.exp2(s - m_new)
        l_ref[:, cols] = alpha * l_ref[:, cols] + jnp.sum(p, axis=0, keepdims=True)
        acc_ref[:, cols] = alpha * acc_ref[:, cols] + jnp.dot(vt, p.astype(vt.dtype), preferred_element_type=F32)
        m_ref[:, cols] = m_new

    def sweep(blocks, kvlen):
        tk = _tile(kvlen, tk_pref)
        n_steps = kvlen // tk
        s_ref[0:tk, :] = scores(blocks[0], k_ref[0:tk, :])

        def body(j, carry):
            off = pl.multiple_of(j * tk, tk)
            nxt = pl.multiple_of(jnp.minimum(j + 1, n_steps - 1) * tk, tk)
            k = k_ref[pl.ds(off, tk), :]
            vt = v_ref[:, pl.ds(off, tk)]
            s = s_ref[0:tk, :]
            for i, c in enumerate(blocks):
                if i + 1 < len(blocks):
                    s_next = scores(blocks[i + 1], k)
                else:
                    s_next = scores(blocks[0], k_ref[pl.ds(nxt, tk), :])
                update(c, s, vt)
                s = s_next
            s_ref[0:tk, :] = s
            return carry

        lax.fori_loop(0, n_steps, body, 0)

    def run(ctx_blocks, full_blocks):
        m_ref[...] = jnp.full(m_ref.shape, -jnp.inf, F32)
        l_ref[...] = jnp.zeros(l_ref.shape, F32)
        acc_ref[...] = jnp.zeros(acc_ref.shape, F32)
        if ctx_blocks:
            sweep(ctx_blocks, ctx)
        if full_blocks:
            sweep(full_blocks, lseq)
        o_ref[...] = (acc_ref[...] / l_ref[...]).T.astype(o_ref.dtype)

    first = pl.program_id(2) == 0

    @pl.when(first)
    def _():
        run(list(range(n_ctx_blocks)), list(range(n_ctx_blocks, n_blocks)))

    @pl.when(jnp.logical_not(first))
    def _():
        run([], list(range(n_blocks)))


def _attention(rg, qt, k, vt, nh, tq_pref=768, tk_pref=768):
    lseq = rg.lseq
    tq = _tile(lseq, tq_pref, ATTN_QBLOCK)
    assert tq >= rg.ctx and rg.ctx % ATTN_QBLOCK == 0 and V_DIM == LANES
    nq = lseq // tq
    blocks = (_nbytes((lseq, 2 * LANES), BF16) + _nbytes((V_DIM, lseq), BF16) + _nbytes((2 * LANES, tq), BF16)
              + _nbytes((tq, V_DIM), BF16))
    return pl.pallas_call(
        functools.partial(_attn_kernel, rg.ctx, tk_pref),
        grid=(rg.nb, nh, nq),
        in_specs=[pl.BlockSpec((None, None, QK_DIM, tq), lambda b, h, i: (b, h, 0, i)),
                  pl.BlockSpec((None, None, lseq, QK_DIM), lambda b, h, i: (b, h, 0, 0)),
                  pl.BlockSpec((None, None, V_DIM, lseq), lambda b, h, i: (b, h, 0, 0))],
        out_specs=pl.BlockSpec((tq, V_DIM), lambda b, h, i: (b * nq + i, h)),
        out_shape=jax.ShapeDtypeStruct((rg.rows, nh * V_DIM), BF16),
        scratch_shapes=[pltpu.VMEM((1, tq), F32), pltpu.VMEM((1, tq), F32), pltpu.VMEM((V_DIM, tq), F32),
                        pltpu.VMEM((tk_pref, ATTN_QBLOCK), F32)],
        compiler_params=_params(("parallel", "parallel", "arbitrary"), blocks),
        name="mla_attention",
    )(qt, k, vt)


def _ssd_kernel(reverse, ngroups, x_ref, b_ref, c_ref, a_ref, dt_ref, extra_ref, y_ref, s_ref):
    q = SSM_CHUNK
    hpg = a_ref.shape[0] // ngroups
    pair_w = 2 * SSM_HEADDIM
    assert pair_w == LANES and SSM_STATE == LANES and hpg % 2 == 0
    gw = hpg * SSM_HEADDIM

    @pl.when(pl.program_id(1) == 0)
    def _():
        s_ref[...] = jnp.zeros(s_ref.shape, F32)

    rows = lax.broadcasted_iota(jnp.int32, (q, q), 0)
    cols = lax.broadcasted_iota(jnp.int32, (q, q), 1)
    causal = (rows <= cols) if reverse else (rows >= cols)
    low = cols < SSM_HEADDIM
    last = 0 if reverse else q - 1

    def group(g, carry):
        goff = pl.multiple_of(g * SSM_STATE, SSM_STATE)
        bg = b_ref[:, pl.ds(goff, SSM_STATE)]
        cg = c_ref[:, pl.ds(goff, SSM_STATE)]
        cb = lax.dot_general(cg, bg, (((1,), (1,)), ((), ())), preferred_element_type=F32)
        bt = bg.astype(F32).T
        hoff = pl.multiple_of(g * hpg, hpg)
        ag = a_ref[pl.ds(hoff, hpg), :]
        dtg = dt_ref[pl.ds(hoff, hpg), :]
        sg = s_ref[g]
        y_off = jnp.dot(cg, sg.astype(BF16), preferred_element_type=F32)
        xoff = pl.multiple_of(g * gw, gw)
        for k in range(hpg // 2):
            top, bot, e_col, e_tot = [], [], [], []
            for hh in (2 * k, 2 * k + 1):
                a_r = ag[hh:hh + 1, :]
                dt_r = dtg[hh:hh + 1, :]
                a_lane = jnp.broadcast_to(a_r, (q, q))
                a_sub = a_lane.T
                decay = jnp.exp(jnp.where(causal, a_sub - a_lane, -jnp.inf))
                top.append(cb * decay * dt_r)
                tot = a_r[:, last:last + 1]
                bot.append(bt * (jnp.exp(tot - a_r) * dt_r))
                e_col.append(jnp.exp(a_sub))
                e_tot.append(jnp.exp(tot))
            lhs = jnp.concatenate([jnp.concatenate(top, axis=1), jnp.concatenate(bot, axis=1)], axis=0)
            xp = x_ref[:, pl.ds(xoff + k * pair_w, pair_w)].astype(F32)
            rhs = jnp.concatenate([jnp.where(low, xp, 0.0), jnp.where(low, 0.0, xp)], axis=0)
            r = jnp.dot(lhs.astype(BF16), rhs.astype(BF16), preferred_element_type=F32)
            lo, hi = k * pair_w, (k + 1) * pair_w
            y = r[0:q, :] + y_off[:, lo:hi] * jnp.where(low, e_col[0], e_col[1])
            s_ref[g, :, lo:hi] = jnp.where(low, e_tot[0], e_tot[1]) * sg[:, lo:hi] + r[q:2 * q, :]
            ext = extra_ref[:, pl.ds(xoff + k * pair_w, pair_w)]
            y = y + (ext if reverse else ext * xp)
            y_ref[:, pl.ds(xoff + k * pair_w, pair_w)] = y.astype(y_ref.dtype)
        return carry

    lax.fori_loop(0, ngroups, group, 0)


def _ssd(rg, xbc, arow, dtrow, extra, reverse, nheads, out_dtype):
    q = SSM_CHUNK
    inner = nheads * SSM_HEADDIM
    gn = SSM_GROUPS * SSM_STATE
    assert inner % gn == 0 and xbc.shape[1] == inner + 2 * gn
    nch = rg.lseq // q
    ncc = rg.ctx // q

    def chunk(b, s):
        c = jnp.where(s < ncc, ncc - 1 - s, nch - 1 - (s - ncc)) if reverse else s
        return b * nch + c

    d = 1 if reverse else 0
    extra_spec = (pl.BlockSpec((q, inner), lambda b, s: (chunk(b, s), 0)) if reverse
                  else pl.BlockSpec((1, inner), lambda b, s: (0, 0)))
    blocks = 3 * _nbytes((q, inner), F32) + _nbytes((SSM_GROUPS, SSM_STATE, inner // SSM_GROUPS), F32)
    return pl.pallas_call(
        functools.partial(_ssd_kernel, reverse, SSM_GROUPS),
        grid=(rg.nb, nch),
        in_specs=[pl.BlockSpec((q, inner), lambda b, s: (chunk(b, s), 0)),
                  pl.BlockSpec((q, gn), lambda b, s: (chunk(b, s), inner // gn)),
                  pl.BlockSpec((q, gn), lambda b, s: (chunk(b, s), inner // gn + 1)),
                  pl.BlockSpec((nheads, q), lambda b, s: (d, chunk(b, s))),
                  pl.BlockSpec((nheads, q), lambda b, s: (d, chunk(b, s))),
                  extra_spec],
        out_specs=pl.BlockSpec((q, inner), lambda b, s: (chunk(b, s), 0)),
        out_shape=jax.ShapeDtypeStruct((rg.rows, inner), out_dtype),
        scratch_shapes=[pltpu.VMEM((SSM_GROUPS, SSM_STATE, inner // SSM_GROUPS), F32)],
        compiler_params=_params(("parallel", "arbitrary"), blocks),
        name="ssd_scan_bwd" if reverse else "ssd_scan_fwd",
    )(xbc, xbc, xbc, arow, dtrow, extra)


def _ssmgate_kernel(y_ref, z_ref, g_ref, o_ref):
    v = y_ref[...].astype(F32) * _silu(z_ref[...].astype(F32))
    o_ref[...] = (_rms(v) * g_ref[...]).astype(o_ref.dtype)


def _ssmgate(rg, y, hz, g):
    inner = y.shape[1]
    row = lambda i: (i, 0)
    return pl.pallas_call(
        _ssmgate_kernel,
        grid=(rg.ntiles,),
        in_specs=[pl.BlockSpec((ROW_TILE, inner), row), pl.BlockSpec((ROW_TILE, inner), row),
                  pl.BlockSpec((1, inner), lambda i: (0, 0))],
        out_specs=pl.BlockSpec((ROW_TILE, inner), row),
        out_shape=jax.ShapeDtypeStruct(y.shape, BF16),
        compiler_params=_params(("parallel",), 5 * _nbytes((ROW_TILE, inner), F32)),
        name="ssm_gate_norm",
    )(y, hz, g.reshape(1, inner))


def _merge_kernel(g0_ref, g1_ref, g2_ref, ya_ref, yb_ref, yc_ref, o_ref):
    def term(g_ref, y_ref):
        return jax.nn.sigmoid(g_ref[...].astype(F32)) * y_ref[...].astype(F32)

    o_ref[...] = (term(g0_ref, ya_ref) + term(g1_ref, yb_ref) + term(g2_ref, yc_ref)).astype(o_ref.dtype)


def _merge(rg, gates, ya, yb, yc):
    d = ya.shape[1]
    tc = _tile(d, 2048)
    nb = d // tc
    gspec = lambda k: pl.BlockSpec((ROW_TILE, tc), lambda i, j, k=k: (i, k * nb + j))
    yspec = pl.BlockSpec((ROW_TILE, tc), lambda i, j: (i, j))
    return pl.pallas_call(
        _merge_kernel,
        grid=(rg.ntiles, nb),
        in_specs=[gspec(0), gspec(1), gspec(2), yspec, yspec, yspec],
        out_specs=yspec,
        out_shape=jax.ShapeDtypeStruct(ya.shape, BF16),
        compiler_params=_params(("parallel", "parallel"), 8 * _nbytes((ROW_TILE, tc), F32)),
        name="merge_branches",
    )(gates, gates, gates, ya, yb, yc)


def _rope_tables(ctx, seq):
    nf = QK_ROPE // 4
    pos = jnp.arange(seq)
    row = (pos // GRID_W).astype(F32)
    col = (pos % GRID_W).astype(F32)
    inv = ROPE_BASE ** (-jnp.arange(nf, dtype=F32) / nf)
    ang = jnp.stack([row[:, None] * inv, col[:, None] * inv], axis=1)
    cos = jnp.cos(ang)
    sin = jnp.sin(ang)
    cos_l = jnp.stack([cos, cos], axis=2).reshape(seq, QK_ROPE)
    sin_l = jnp.stack([-sin, sin], axis=2).reshape(seq, QK_ROPE)
    pad = jnp.zeros((seq, LANES - QK_ROPE), F32)
    cos_x = jnp.concatenate([cos_l, pad], axis=1)
    sin_x = jnp.concatenate([sin_l, pad], axis=1)
    cos_c = jnp.concatenate([jnp.ones((ctx, QK_ROPE), F32), jnp.zeros((ctx, LANES - QK_ROPE), F32)], axis=1)
    sin_c = jnp.zeros((ctx, LANES), F32)
    return jnp.concatenate([cos_c, cos_x], axis=0), jnp.concatenate([sin_c, sin_x], axis=0)


def _half_swap_perm():
    nf = QK_ROPE // 4
    idx = np.arange(QK_ROPE).reshape(2, 2, nf)
    return idx[:, ::-1, :].reshape(-1)


def kernel(x, c, ctx, c_ctx, w_ada, b_ada, w_in, w_sc_conv, w_sc_out, g_qa, w_uq, g_kva, w_ukv, w_mla_out,
           w_ssm_conv, b_ssm_conv, dt_bias_f, dt_bias_b, a_log_f, a_log_b, d_skip, g_ssm_norm, w_ssm_out, w_out,
           ln1_g, ln1_b, w_up, w_ff_conv, w_down, ln2_g, ln2_b):
    n_batch, seq, d = x.shape
    n_ctx = ctx.shape[1]
    depth = w_in.shape[0]
    nh = MLA_HEADS
    inner = w_ssm_out.shape[1]
    nheads = inner // SSM_HEADDIM
    gn = SSM_GROUPS * SSM_STATE
    d_ff = w_down.shape[1]
    alpha = (2 * depth) ** 0.25
    rg = _Rows(n_batch, n_ctx, seq)
    assert n_batch < 8

    splits = (d, d, d, Q_LORA, KV_LORA, QK_ROPE, inner, inner, gn, gn, 2 * nheads, N_BRANCH * d)
    o = np.concatenate([[0], np.cumsum(splits)])
    assert o[-1] == w_in.shape[2]
    swap = _half_swap_perm()

    cos_t, sin_t = _rope_tables(n_ctx, seq)

    cond8 = jnp.zeros((8, d), F32).at[0:n_batch].set(c).at[n_batch].set(c_ctx)
    mod_all = _ada(cond8, w_ada, b_ada)
    mod_all = mod_all.reshape(depth, 8, 6, d).transpose(0, 2, 1, 3)[:, :, :, None, :]

    h = jnp.concatenate([ctx, x], axis=1).reshape(rg.rows, d)
    u = _lnmod(rg, h, mod_all[0], 0, 1)

    for l in range(depth):
        mod = mod_all[l]
        wl = w_in[l]
        w_a = wl[:, o[0]:o[3]].astype(BF16)
        w_b = wl[:, o[6]:o[10]].astype(BF16)
        w_c = wl[:, o[11]:o[12]].astype(BF16)
        kr = wl[:, o[5]:o[6]]
        w_d = jnp.concatenate([wl[:, o[3]:o[5]], kr, kr[:, swap], wl[:, o[10]:o[11]]], axis=1).astype(BF16)
        wq3 = w_uq[l].reshape(Q_LORA, nh, QK_DIM)
        pe = wq3[:, :, QK_NOPE:]
        wq = jnp.concatenate([wq3[:, :, :QK_NOPE].reshape(Q_LORA, nh * QK_NOPE),
                              jnp.concatenate([pe, pe[:, :, swap]], axis=2).reshape(Q_LORA, nh * LANES)],
                             axis=1).astype(BF16)
        wkv = w_ukv[l].astype(BF16)

        h_a = _matmul(u, w_a, BF16, "in_proj_conv")
        h_b = _matmul(u, w_b, BF16, "in_proj_ssm")
        h_c = _matmul(u, w_c, BF16, "in_proj_gates")
        lat = _matmul(u, w_d, F32, "in_proj_latent", tn_pref=1280)

        ga = _conv3(rg, [(h_a, d), (h_a, 2 * d)], w_sc_conv[l], d, "sconv_mix", post=(h_a, 0))
        ya = _matmul(ga, w_sc_out[l].astype(BF16), BF16, "sconv_out")

        dt_bias = jnp.concatenate([dt_bias_f[l], dt_bias_b[l]]).reshape(1, 2 * nheads)
        a_log = jnp.concatenate([a_log_f[l], a_log_b[l]]).reshape(1, 2 * nheads)
        cq, ckv, kpe, arow, dtrow = _latprep(rg, lat, g_qa[l], g_kva[l], cos_t, sin_t, dt_bias, a_log, nheads)
        q = _qproj(rg, cq, wq, cos_t, sin_t, nh, (QK_DIM ** -0.5) * LOG2E)
        k, v = _kvproj(rg, ckv, wkv, kpe, nh)
        att = _attention(rg, q, k, v, nh)
        yb = _matmul(att, w_mla_out[l].astype(BF16), BF16, "mla_out")

        xbc = _conv3(rg, [(h_b, inner)], w_ssm_conv[l], inner + 2 * gn, "ssm_conv", bias=b_ssm_conv[l],
                     use_silu=True)
        skip = jnp.repeat(d_skip[l], SSM_HEADDIM).reshape(1, inner)
        y_f = _ssd(rg, xbc, arow, dtrow, skip, False, nheads, F32)
        y_s = _ssd(rg, xbc, arow, dtrow, y_f, True, nheads, BF16)
        gc = _ssmgate(rg, y_s, h_b, g_ssm_norm[l])
        yc = _matmul(gc, w_ssm_out[l].astype(BF16), BF16, "ssm_out")

        merged = _merge(rg, h_c, ya, yb, yc)
        y1 = _matmul(merged, w_out[l].astype(BF16), F32, "mix_out")
        h, u2 = _postnorm(rg, alpha, h, y1, mod, 2, ln1_g[l], ln1_b[l], mod, 3, 4)

        h_f = _matmul(u2, w_up[l].astype(BF16), BF16, "ffn_up")
        gf = _conv3(rg, [(h_f, 0)], w_ff_conv[l], d_ff, "ffn_conv", use_silu=True, post=(h_f, d_ff))
        y2 = _matmul(gf, w_down[l].astype(BF16), F32, "ffn_down", tn_pref=512)
        if l + 1 < depth:
            h, u = _postnorm(rg, alpha, h, y2, mod, 5, ln2_g[l], ln2_b[l], mod_all[l + 1], 0, 1)
        else:
            h, _ = _postnorm(rg, alpha, h, y2, mod, 5, ln2_g[l], ln2_b[l])

    return h.reshape(n_batch, rg.lseq, d)[:, n_ctx:, :]
```
